```python
import math
import jax, jax.numpy as jnp
from jax import lax
import numpy as np

D_MODEL = 1024
BATCH = 8
SEQ = 4096
DEPTH = 2

HEAD_DIM = 64
GRID_W = 64
Q_BLOCK = 128
ROPE_THETA = 10000.0
MLA_HEADS = 6
MLA_Q_RANK = 256
MLA_KV_RANK = 128
MLA_NOPE_DIM = 64
MLA_ROPE_DIM = 32
MLA_V_DIM = 64
DIL_HEADS = 6
DIL_BRANCHES = ((128, 1), (512, 4), (2048, 16))
GQA_Q_HEADS = 4
GQA_KV_HEADS = 2
REL_BUCKETS = 32
REL_MAX_DIST = 1024
D_FF = -(-8 * D_MODEL // (3 * 256)) * 256
MLA_IN = MLA_Q_RANK + MLA_KV_RANK + MLA_ROPE_DIM
DIL_IN = 3 * DIL_HEADS * HEAD_DIM
GQA_IN = (GQA_Q_HEADS + 2 * GQA_KV_HEADS) * HEAD_DIM
IN_WIDTH = MLA_IN + DIL_IN + GQA_IN
MIX_WIDTH = MLA_HEADS * MLA_V_DIM + DIL_HEADS * HEAD_DIM + GQA_Q_HEADS * HEAD_DIM
DN_ALPHA = (2.0 * DEPTH) ** 0.25
DN_BETA = (8.0 * DEPTH) ** -0.25
NEG_INF = -1e30

kernel_name = "hybrid_mla_dilated_axialgqa_deepnorm_encoder"


def rms_norm(x, g, eps=1e-6):
    xf = x.astype(jnp.float32)
    y = xf * lax.rsqrt(jnp.mean(xf * xf, axis=-1, keepdims=True) + eps)
    return (y * g.astype(jnp.float32)).astype(x.dtype)


def layer_norm(x, g, b, eps=1e-5):
    xf = x.astype(jnp.float32)
    mu = jnp.mean(xf, axis=-1, keepdims=True)
    var = jnp.mean(jnp.square(xf - mu), axis=-1, keepdims=True)
    y = (xf - mu) * lax.rsqrt(var + eps)
    return (y * g.astype(jnp.float32) + b.astype(jnp.float32)).astype(x.dtype)


def rope(x, pos):
    d = x.shape[-1]
    inv = ROPE_THETA ** (-jnp.arange(0, d, 2, dtype=jnp.float32) / d)
    ang = pos[:, None] * inv[None, :]
    cos = jnp.cos(ang)[None, :, None, :]
    sin = jnp.sin(ang)[None, :, None, :]
    xf = x.astype(jnp.float32)
    x1, x2 = xf[..., : d // 2], xf[..., d // 2:]
    return jnp.concatenate([x1 * cos - x2 * sin, x1 * sin + x2 * cos], axis=-1).astype(x.dtype)


def t5_bucket(rel):
    nb = REL_BUCKETS // 2
    exact = nb // 2
    ret = jnp.where(rel > 0, nb, 0)
    n = jnp.abs(rel)
    nf = jnp.maximum(n, 1).astype(jnp.float32)
    large = exact + (jnp.log(nf / exact) / math.log(REL_MAX_DIST / exact) * (nb - exact)).astype(jnp.int32)
    large = jnp.minimum(large, nb - 1)
    return ret + jnp.where(n < exact, n, large)


def dense_attention(q, k, v, scale):
    B, S, H, Dk = q.shape
    Hkv, Dv = k.shape[2], v.shape[-1]
    G = H // Hkv
    nblk = S // Q_BLOCK
    qb = q.reshape(B, nblk, Q_BLOCK, Hkv, G, Dk).transpose(1, 0, 2, 3, 4, 5)

    def one_block(qblk):
        logits = jnp.einsum('bqkgd,bskd->bkgqs', qblk, k).astype(jnp.float32) * scale
        p = jax.nn.softmax(logits, axis=-1)
        return jnp.einsum('bkgqs,bskd->bqkgd', p.astype(v.dtype), v)

    o = lax.map(one_block, qb)
    return o.transpose(1, 0, 2, 3, 4, 5).reshape(B, S, H, Dv)


def dilated_branch(q, k, v, rel_bias, window, dil):
    B, S, H, D = q.shape
    half = window // (2 * dil)
    L = S // dil
    nb = -(-L // half)
    Lp = nb * half
    BB = B * dil

    def to_sub(t):
        return t.reshape(B, L, dil, H, D).transpose(0, 2, 1, 3, 4).reshape(BB, L, H, D)

    qs, ks, vs = to_sub(q), to_sub(k), to_sub(v)
    qb = jnp.pad(qs, ((0, 0), (0, Lp - L), (0, 0), (0, 0))).reshape(BB, nb, half, H, D)

    def band(t):
        tp = jnp.pad(t, ((0, 0), (half, Lp - L + half), (0, 0), (0, 0))).reshape(BB, nb + 2, half, H, D)
        return jnp.concatenate([tp[:, :-2], tp[:, 1:-1], tp[:, 2:]], axis=2)

    kb, vb = band(ks), band(vs)
    rel = jnp.arange(3 * half)[None, :] - half - jnp.arange(half)[:, None]
    bias = jnp.transpose(rel_bias[t5_bucket(rel * dil)], (2, 0, 1)).astype(jnp.float32)
    key_idx = jnp.arange(nb)[:, None] * half - half + jnp.arange(3 * half)[None, :]
    mask = (jnp.abs(rel) <= half)[None] & ((key_idx >= 0) & (key_idx < L))[:, None, :]

    logits = jnp.einsum('bnqhd,bnkhd->bnhqk', qb, kb).astype(jnp.float32) * (HEAD_DIM ** -0.5)
    logits = jnp.where(mask[None, :, None], logits + bias[None, None], NEG_INF)
    m = jnp.max(logits, axis=-1, keepdims=True)
    e = jnp.exp(logits - m)
    s = jnp.sum(e, axis=-1)
    o = jnp.einsum('bnhqk,bnkhd->bnqhd', e.astype(v.dtype), vb).astype(jnp.float32)
    o = o / jnp.transpose(s, (0, 1, 3, 2))[..., None]
    lse = jnp.transpose(m[..., 0] + jnp.log(s), (0, 1, 3, 2))
    o = o.reshape(BB, Lp, H, D)[:, :L]
    lse = lse.reshape(BB, Lp, H)[:, :L]
    o = o.reshape(B, dil, L, H, D).transpose(0, 2, 1, 3, 4).reshape(B, S, H, D)
    lse = lse.reshape(B, dil, L, H).transpose(0, 2, 1, 3).reshape(B, S, H)
    return o, lse


def dilated_mixture(q, k, v, rel_bias):
    outs, lses = [], []
    for window, dil in DIL_BRANCHES:
        o, lse = dilated_branch(q, k, v, rel_bias, window, dil)
        outs.append(o)
        lses.append(lse)
    w = jax.nn.softmax(jnp.stack(lses, axis=0), axis=0)
    o = jnp.sum(w[..., None] * jnp.stack(outs, axis=0), axis=0)
    return o.astype(q.dtype)


def setup_inputs(seed: int = 0) -> dict:
    key = jax.random.key(seed)
    ks = jax.random.split(key, 20)
    f32 = jnp.float32

    def nrm(k, shape, scale):
        return jax.random.normal(k, shape, f32) * scale

    def gain(k, shape):
        return 1.0 + 0.02 * jax.random.normal(k, shape, f32)

    return {
        "x": jax.random.normal(ks[0], (BATCH, SEQ, D_MODEL), f32),
        "w_in": nrm(ks[1], (DEPTH, D_MODEL, IN_WIDTH), D_MODEL ** -0.5),
        "mla_q_norm": gain(ks[2], (DEPTH, MLA_Q_RANK)),
        "mla_kv_norm": gain(ks[3], (DEPTH, MLA_KV_RANK)),
        "mla_w_uq": nrm(ks[4], (DEPTH, MLA_Q_RANK, MLA_HEADS * (MLA_NOPE_DIM + MLA_ROPE_DIM)), MLA_Q_RANK ** -0.5),
        "mla_w_ukv": nrm(ks[5], (DEPTH, MLA_KV_RANK, MLA_HEADS * (MLA_NOPE_DIM + MLA_V_DIM)), MLA_KV_RANK ** -0.5),
        "gqa_q_norm": gain(ks[6], (DEPTH, HEAD_DIM)),
        "gqa_k_norm": gain(ks[7], (DEPTH, HEAD_DIM)),
        "rel_bias": nrm(ks[8], (REL_BUCKETS, DIL_HEADS), 0.1),
        "w_out": nrm(ks[9], (DEPTH, MIX_WIDTH, D_MODEL), DN_BETA * MIX_WIDTH ** -0.5),
        "ln1_g": gain(ks[10], (DEPTH, D_MODEL)),
        "ln1_b": nrm(ks[11], (DEPTH, D_MODEL), 0.02),
        "ffn_w_gate": nrm(ks[12], (DEPTH, D_MODEL, D_FF), D_MODEL ** -0.5),
        "ffn_w_up": nrm(ks[13], (DEPTH, D_MODEL, D_FF), D_MODEL ** -0.5),
        "ffn_w_down": nrm(ks[14], (DEPTH, D_FF, D_MODEL), DN_BETA * D_FF ** -0.5),
        "ln2_g": gain(ks[15], (DEPTH, D_MODEL)),
        "ln2_b": nrm(ks[16], (DEPTH, D_MODEL), 0.02),
    }


def reference(x, w_in, mla_q_norm, mla_kv_norm, mla_w_uq, mla_w_ukv, gqa_q_norm, gqa_k_norm,
              rel_bias, w_out, ln1_g, ln1_b, ffn_w_gate, ffn_w_up, ffn_w_down, ln2_g, ln2_b):
    B, S, _ = x.shape
    rows = S // GRID_W
    pos = jnp.arange(S, dtype=jnp.float32)
    row_pos = jnp.repeat(jnp.arange(rows), GRID_W).astype(jnp.float32)
    col_pos = jnp.tile(jnp.arange(GRID_W), rows).astype(jnp.float32)
    half_rot = HEAD_DIM // 2

    for l in range(DEPTH):
        h = x @ w_in[l]
        o0 = 0
        cq = rms_norm(h[..., o0:o0 + MLA_Q_RANK], mla_q_norm[l]); o0 += MLA_Q_RANK
        ckv = rms_norm(h[..., o0:o0 + MLA_KV_RANK], mla_kv_norm[l]); o0 += MLA_KV_RANK
        k_rope = h[..., o0:o0 + MLA_ROPE_DIM][:, :, None, :]; o0 += MLA_ROPE_DIM
        qa = (cq @ mla_w_uq[l]).reshape(B, S, MLA_HEADS, MLA_NOPE_DIM + MLA_ROPE_DIM)
        qa = jnp.concatenate([qa[..., :MLA_NOPE_DIM], rope(qa[..., MLA_NOPE_DIM:], pos)], axis=-1)
        kva = (ckv @ mla_w_ukv[l]).reshape(B, S, MLA_HEADS, MLA_NOPE_DIM + MLA_V_DIM)
        k_rope = jnp.broadcast_to(rope(k_rope, pos), (B, S, MLA_HEADS, MLA_ROPE_DIM))
        ka = jnp.concatenate([kva[..., :MLA_NOPE_DIM], k_rope], axis=-1)
        va = kva[..., MLA_NOPE_DIM:]
        out_a = dense_attention(qa, ka, va, (MLA_NOPE_DIM + MLA_ROPE_DIM) ** -0.5)
        out_a = out_a.reshape(B, S, MLA_HEADS * MLA_V_DIM)
        hb = h[..., o0:o0 + DIL_IN].reshape(B, S, 3, DIL_HEADS, HEAD_DIM); o0 += DIL_IN
        out_b = dilated_mixture(hb[:, :, 0], hb[:, :, 1], hb[:, :, 2], rel_bias)
        out_b = out_b.reshape(B, S, DIL_HEADS * HEAD_DIM)
        nq, nkv = GQA_Q_HEADS * HEAD_DIM, GQA_KV_HEADS * HEAD_DIM
        qc = h[..., o0:o0 + nq].reshape(B, S, GQA_Q_HEADS, HEAD_DIM); o0 += nq
        kc = h[..., o0:o0 + nkv].reshape(B, S, GQA_KV_HEADS, HEAD_DIM); o0 += nkv
        vc = h[..., o0:o0 + nkv].reshape(B, S, GQA_KV_HEADS, HEAD_DIM); o0 += nkv
        qc = rms_norm(qc, gqa_q_norm[l])
        kc = rms_norm(kc, gqa_k_norm[l])
        qc = jnp.concatenate([rope(qc[..., :half_rot], row_pos), rope(qc[..., half_rot:], col_pos)], axis=-1)
        kc = jnp.concatenate([rope(kc[..., :half_rot], row_pos), rope(kc[..., half_rot:], col_pos)], axis=-1)
        out_c = dense_attention(qc, kc, vc, HEAD_DIM ** -0.5).reshape(B, S, GQA_Q_HEADS * HEAD_DIM)
        mix = jnp.concatenate([out_a, out_b, out_c], axis=-1) @ w_out[l]
        x = layer_norm(DN_ALPHA * x + mix, ln1_g[l], ln1_b[l])
        ff = (jax.nn.silu(x @ ffn_w_gate[l]) * (x @ ffn_w_up[l])) @ ffn_w_down[l]
        x = layer_norm(DN_ALPHA * x + ff, ln2_g[l], ln2_b[l])
    return x
```

```python
import functools
import math

import jax
import jax.numpy as jnp
from jax import lax
from jax.experimental import pallas as pl
from jax.experimental.pallas import tpu as pltpu

F32 = jnp.float32
BF16 = jnp.bfloat16

D_MODEL = 1024
HEAD_DIM = 64
GRID_W = 64
ROPE_THETA = 10000.0
MLA_HEADS = 6
MLA_Q_RANK = 256
MLA_KV_RANK = 128
MLA_NOPE_DIM = 64
MLA_ROPE_DIM = 32
MLA_V_DIM = 64
DIL_HEADS = 6
DIL_BRANCHES = ((128, 1), (512, 4), (2048, 16))
GQA_Q_HEADS = 4
GQA_KV_HEADS = 2
REL_BUCKETS = 32
REL_MAX_DIST = 1024
D_FF = 2816
MLA_IN = MLA_Q_RANK + MLA_KV_RANK + MLA_ROPE_DIM
DIL_IN = 3 * DIL_HEADS * HEAD_DIM
GQA_IN = (GQA_Q_HEADS + 2 * GQA_KV_HEADS) * HEAD_DIM
NEG_INF = -1e30
LOG2E = math.log2(math.e)

V7X_LANES = 128
V7X_VMEM_LIMIT_BYTES = 56 * 1024 * 1024

KPAD = V7X_LANES

PREP_TM = 512
ATT_TQ = 256
ATT_TK = 256
DIL_TQ = 256
DIL_HALO = 128
OUT_TM = 512
FFN_TM = 1024
FFN_TF = 256


def _cparams(n_axes):
    return pltpu.CompilerParams(
        dimension_semantics=("arbitrary",) * n_axes,
        vmem_limit_bytes=V7X_VMEM_LIMIT_BYTES,
    )


def _rms_t(h, g, eps=1e-6):
    ms = jnp.mean(h * h, axis=0, keepdims=True)
    return h * lax.rsqrt(ms + eps) * g


def _prep_kernel(x_ref, w_ref, wuq_ref, wukv_ref, gq_ref, gkv_ref, gcq_ref, gck_ref, tab_ref,
                 qa_ref, ka_ref, va_ref, qb_ref, kb_ref, vb_ref, qc_ref, kc_ref, vc_ref, *, dtype_k):
    xb = x_ref[0].astype(BF16)
    tm = xb.shape[0]

    def proj(r0, r1):
        return lax.dot_general(w_ref[r0:r1, :], xb, (((1,), (1,)), ((), ())),
                               preferred_element_type=F32)

    cos_p, sin_p = tab_ref[0], tab_ref[1]
    cos_r, sin_r = tab_ref[2], tab_ref[3]
    cos_c, sin_c = tab_ref[4], tab_ref[5]
    zpad_bf = jnp.zeros((KPAD - 96, tm), BF16)
    zpad_k = jnp.zeros((KPAD - 96, tm), dtype_k)

    sc_a = (MLA_NOPE_DIM + MLA_ROPE_DIM) ** -0.5 * LOG2E
    hq = proj(0, MLA_Q_RANK)
    cq = _rms_t(hq, gq_ref[...]).astype(BF16)
    qa = jnp.dot(wuq_ref[...], cq, preferred_element_type=F32)
    hkv = proj(MLA_Q_RANK, MLA_Q_RANK + MLA_KV_RANK)
    ckv = _rms_t(hkv, gkv_ref[...]).astype(BF16)
    kva = jnp.dot(wukv_ref[...], ckv, preferred_element_type=F32)
    hkr = proj(MLA_Q_RANK + MLA_KV_RANK, MLA_IN)
    kr1, kr2 = hkr[0:16], hkr[16:32]
    krope1 = (kr1 * cos_p - kr2 * sin_p).astype(dtype_k)
    krope2 = (kr1 * sin_p + kr2 * cos_p).astype(dtype_k)
    for h in range(MLA_HEADS):
        qb0 = 96 * h
        ob = KPAD * h
        r1, r2 = qa[qb0 + 64:qb0 + 80], qa[qb0 + 80:qb0 + 96]
        qa_ref[0, ob:ob + 64, :] = (qa[qb0:qb0 + 64] * sc_a).astype(BF16)
        qa_ref[0, ob + 64:ob + 80, :] = ((r1 * cos_p - r2 * sin_p) * sc_a).astype(BF16)
        qa_ref[0, ob + 80:ob + 96, :] = ((r1 * sin_p + r2 * cos_p) * sc_a).astype(BF16)
        qa_ref[0, ob + 96:ob + KPAD, :] = zpad_bf
        kb0 = 128 * h
        ka_ref[0, ob:ob + 64, :] = kva[kb0:kb0 + 64].astype(dtype_k)
        ka_ref[0, ob + 64:ob + 80, :] = krope1
        ka_ref[0, ob + 80:ob + 96, :] = krope2
        ka_ref[0, ob + 96:ob + KPAD, :] = zpad_k
        va_ref[0, 64 * h:64 * h + 64, :] = kva[kb0 + 64:kb0 + 128].astype(BF16)

    sc_b = HEAD_DIM ** -0.5 * LOG2E
    o0 = MLA_IN
    nb = DIL_HEADS * HEAD_DIM
    hbq = proj(o0, o0 + nb)
    hbk = proj(o0 + nb, o0 + 2 * nb)
    hbv = proj(o0 + 2 * nb, o0 + 3 * nb)
    z64_bf = jnp.zeros((KPAD - 64, tm), BF16)
    z64_k = jnp.zeros((KPAD - 64, tm), dtype_k)
    for h in range(DIL_HEADS):
        ob = KPAD * h
        qb_ref[0, ob:ob + 64, :] = (hbq[64 * h:64 * h + 64] * sc_b).astype(BF16)
        qb_ref[0, ob + 64:ob + KPAD, :] = z64_bf
        kb_ref[0, ob:ob + 64, :] = hbk[64 * h:64 * h + 64].astype(dtype_k)
        kb_ref[0, ob + 64:ob + KPAD, :] = z64_k
    vb_ref[0] = hbv.astype(BF16)

    sc_c = HEAD_DIM ** -0.5 * LOG2E
    o0 = MLA_IN + DIL_IN
    nqk = (GQA_Q_HEADS + GQA_KV_HEADS) * HEAD_DIM
    hqk = proj(o0, o0 + nqk)
    hv = proj(o0 + nqk, o0 + GQA_IN)

    def norm_rope(xh, g):
        y = _rms_t(xh, g)
        a1, a2, b1, b2 = y[0:16], y[16:32], y[32:48], y[48:64]
        return (a1 * cos_r - a2 * sin_r, a1 * sin_r + a2 * cos_r,
                b1 * cos_c - b2 * sin_c, b1 * sin_c + b2 * cos_c)

    group = GQA_Q_HEADS // GQA_KV_HEADS
    for j in range(GQA_Q_HEADS):
        parts = norm_rope(hqk[64 * j:64 * j + 64], gcq_ref[...])
        g = j // group
        for gg in range(GQA_KV_HEADS):
            ob = KPAD * j + 64 * gg
            if gg == g:
                for t, part in enumerate(parts):
                    qc_ref[0, ob + 16 * t:ob + 16 * t + 16, :] = (part * sc_c).astype(BF16)
            else:
                qc_ref[0, ob:ob + 64, :] = z64_bf
    for j in range(GQA_KV_HEADS):
        r0 = 64 * (GQA_Q_HEADS + j)
        parts = norm_rope(hqk[r0:r0 + 64], gck_ref[...])
        for t, part in enumerate(parts):
            kc_ref[0, 64 * j + 16 * t:64 * j + 16 * t + 16, :] = part.astype(dtype_k)
    vc_ref[0] = hv.astype(BF16)


def _prep(x, w_in_t, wuq_t, wukv_t, gq, gkv, gcq, gck, tabs):
    B, S, D = x.shape
    tm = PREP_TM
    dtype_k = F32
    full = lambda shape: pl.BlockSpec(shape, lambda b, i: (0,) * len(shape))
    tok = lambda rows: pl.BlockSpec((1, rows, tm), lambda b, i: (b, 0, i))
    out_rows = [(MLA_HEADS * KPAD, BF16), (MLA_HEADS * KPAD, dtype_k), (MLA_HEADS * MLA_V_DIM, BF16),
                (DIL_HEADS * KPAD, BF16), (DIL_HEADS * KPAD, dtype_k), (DIL_HEADS * HEAD_DIM, BF16),
                (GQA_Q_HEADS * KPAD, BF16), (GQA_KV_HEADS * HEAD_DIM, dtype_k), (GQA_KV_HEADS * HEAD_DIM, BF16)]
    return pl.pallas_call(
        functools.partial(_prep_kernel, dtype_k=dtype_k),
        grid=(B, S // tm),
        in_specs=[pl.BlockSpec((1, tm, D), lambda b, i: (b, i, 0)),
                  full(w_in_t.shape), full(wuq_t.shape), full(wukv_t.shape),
                  full(gq.shape), full(gkv.shape), full(gcq.shape), full(gck.shape),
                  pl.BlockSpec((6, 16, tm), lambda b, i: (0, 0, i))],
        out_specs=[tok(r) for r, _ in out_rows],
        out_shape=[jax.ShapeDtypeStruct((B, r, S), dt) for r, dt in out_rows],
        compiler_params=_cparams(2),
        name="prep",
    )(x, w_in_t, wuq_t, wukv_t, gq, gkv, gcq, gck, tabs)


def _flash_kernel(q_ref, k_ref, v_ref, o_ref, knat_ref, *, seq, tk, shared_k):
    h = pl.program_id(1)
    i = pl.program_id(2)
    first = (i == 0) & (h == 0) if shared_k else (i == 0)

    @pl.when(first)
    def _():
        chunk = 512
        for c in range(seq // chunk):
            knat_ref[c * chunk:(c + 1) * chunk, :] = k_ref[0, :, c * chunk:(c + 1) * chunk].T.astype(BF16)

    q = q_ref[0]
    tq = q.shape[1]
    dv = v_ref.shape[1]

    def body(j, carry):
        m, l, acc = carry
        off = pl.multiple_of(j * tk, tk)
        s = jnp.dot(knat_ref[pl.ds(off, tk), :], q, preferred_element_type=F32)
        m_new = jnp.maximum(m, jnp.max(s, axis=0, keepdims=True))
        alpha = jnp.exp2(m - m_new)
        p = jnp.exp2(s - m_new)
        l = alpha * l + jnp.sum(p, axis=0, keepdims=True)
        acc = alpha * acc + jnp.dot(v_ref[0, :, pl.ds(off, tk)], p.astype(BF16),
                                    preferred_element_type=F32)
        return m_new, l, acc

    init = (jnp.full((1, tq), NEG_INF, F32), jnp.zeros((1, tq), F32), jnp.zeros((dv, tq), F32))
    m, l, acc = lax.fori_loop(0, seq // tk, body, init)
    o_ref[0] = (acc * (1.0 / l)).astype(o_ref.dtype)


def _flash(q_t, k_t, v_t, n_heads, n_kv, shared_k):
    B, _, S = q_t.shape
    dv = v_t.shape[1] // n_kv
    group = n_heads // n_kv
    tq = ATT_TQ
    k_idx = (lambda b, h, i: (b, 0, 0)) if shared_k else (lambda b, h, i: (b, h // group, 0))
    return pl.pallas_call(
        functools.partial(_flash_kernel, seq=S, tk=ATT_TK, shared_k=shared_k),
        grid=(B, n_heads, S // tq),
        in_specs=[pl.BlockSpec((1, KPAD, tq), lambda b, h, i: (b, h, i)),
                  pl.BlockSpec((1, KPAD, S), k_idx),
                  pl.BlockSpec((1, dv, S), lambda b, h, i: (b, h // group, 0))],
        out_specs=pl.BlockSpec((1, dv, tq), lambda b, h, i: (b, h, i)),
        out_shape=jax.ShapeDtypeStruct((B, n_heads * dv, S), BF16),
        scratch_shapes=[pltpu.VMEM((S, KPAD), BF16)],
        compiler_params=_cparams(3),
        name="flash_shared" if shared_k else "flash",
    )(q_t, k_t, v_t)


def _dil_kernel(q_ref, k_ref, v_ref, bias_ref, o_ref, lse_ref, knat_ref, vpad_ref, *, sub_len):
    i = pl.program_id(2)
    tq = q_ref.shape[2]
    win = tq + 2 * DIL_HALO
    dv = v_ref.shape[1]

    @pl.when(i == 0)
    def _():
        knat_ref[0:DIL_HALO, :] = jnp.zeros((DIL_HALO, KPAD), BF16)
        knat_ref[DIL_HALO + sub_len:2 * DIL_HALO + sub_len, :] = jnp.zeros((DIL_HALO, KPAD), BF16)
        chunk = 256
        for c in range(sub_len // chunk):
            knat_ref[DIL_HALO + c * chunk:DIL_HALO + (c + 1) * chunk, :] = (
                k_ref[0, :, c * chunk:(c + 1) * chunk].T.astype(BF16))
        vpad_ref[:, 0:DIL_HALO] = jnp.zeros((dv, DIL_HALO), BF16)
        vpad_ref[:, DIL_HALO + sub_len:2 * DIL_HALO + sub_len] = jnp.zeros((dv, DIL_HALO), BF16)
        vpad_ref[:, DIL_HALO:DIL_HALO + sub_len] = v_ref[0]

    l0 = pl.multiple_of(i * tq, tq)
    s = jnp.dot(knat_ref[pl.ds(l0, win), :], q_ref[0], preferred_element_type=F32)
    key = l0 - DIL_HALO + lax.broadcasted_iota(jnp.int32, (win, tq), 0)
    valid = (key >= 0) & (key < sub_len)
    s = jnp.where(valid, s + bias_ref[0], NEG_INF)
    m = jnp.max(s, axis=0, keepdims=True)
    p = jnp.exp2(s - m)
    l = jnp.sum(p, axis=0, keepdims=True)
    o = jnp.dot(vpad_ref[:, pl.ds(l0, win)], p.astype(BF16), preferred_element_type=F32)
    o_ref[0] = o * (1.0 / l)
    lse_ref[0, 0] = m + jnp.log2(l)


def _dil_branch(q_t, k_t, v_t, bias_t):
    BB, _, L = q_t.shape
    tq = DIL_TQ
    win = tq + 2 * DIL_HALO
    H = DIL_HEADS
    return pl.pallas_call(
        functools.partial(_dil_kernel, sub_len=L),
        grid=(BB, H, L // tq),
        in_specs=[pl.BlockSpec((1, KPAD, tq), lambda b, h, i: (b, h, i)),
                  pl.BlockSpec((1, KPAD, L), lambda b, h, i: (b, h, 0)),
                  pl.BlockSpec((1, HEAD_DIM, L), lambda b, h, i: (b, h, 0)),
                  pl.BlockSpec((1, win, tq), lambda b, h, i: (h, 0, 0))],
        out_specs=[pl.BlockSpec((1, HEAD_DIM, tq), lambda b, h, i: (b, h, i)),
                   pl.BlockSpec((1, 1, 1, tq), lambda b, h, i: (b, h, 0, i))],
        out_shape=[jax.ShapeDtypeStruct((BB, H * HEAD_DIM, L), F32),
                   jax.ShapeDtypeStruct((BB, H, 1, L), F32)],
        scratch_shapes=[pltpu.VMEM((L + 2 * DIL_HALO, KPAD), BF16),
                        pltpu.VMEM((HEAD_DIM, L + 2 * DIL_HALO), BF16)],
        compiler_params=_cparams(3),
        name="dilated",
    )(q_t, k_t, v_t, bias_t)


def _t5_bucket(rel):
    nb = REL_BUCKETS // 2
    exact = nb // 2
    ret = jnp.where(rel > 0, nb, 0)
    n = jnp.abs(rel)
    nf = jnp.maximum(n, 1).astype(F32)
    large = exact + (jnp.log(nf / exact) / math.log(REL_MAX_DIST / exact) * (nb - exact)).astype(jnp.int32)
    large = jnp.minimum(large, nb - 1)
    return ret + jnp.where(n < exact, n, large)


def _dil_bias_table(rel_bias, window, dil):
    half = window // (2 * dil)
    tq = DIL_TQ
    win = tq + 2 * DIL_HALO
    rel = jnp.arange(win)[:, None] - DIL_HALO - jnp.arange(tq)[None, :]
    bias = jnp.transpose(rel_bias[_t5_bucket(rel * dil)], (2, 0, 1)).astype(F32) * LOG2E
    return jnp.where((jnp.abs(rel) <= half)[None], bias, NEG_INF)


def _to_sub(t, dil):
    if dil == 1:
        return t
    B, R, S = t.shape
    return t.reshape(B, R, S // dil, dil).transpose(0, 3, 1, 2).reshape(B * dil, R, S // dil)


def _from_sub(t, dil):
    if dil == 1:
        return t
    BB, R, L = t.shape
    return t.reshape(BB // dil, dil, R, L).transpose(0, 2, 3, 1).reshape(BB // dil, R, L * dil)


def _layer_norm_rows(y, g, b, eps=1e-5):
    mu = jnp.mean(y, axis=-1, keepdims=True)
    yc = y - mu
    var = jnp.mean(yc * yc, axis=-1, keepdims=True)
    return yc * lax.rsqrt(var + eps) * g + b


def _outproj_kernel(a_ref, c_ref, o1_ref, o2_ref, o3_ref, l1_ref, l2_ref, l3_ref, x_ref, w_ref, g_ref, b_ref,
                    out_ref, *, alpha):
    l1, l2, l3 = l1_ref[0], l2_ref[0], l3_ref[0]
    mx = jnp.maximum(jnp.maximum(l1, l2), l3)
    e1, e2, e3 = jnp.exp2(l1 - mx), jnp.exp2(l2 - mx), jnp.exp2(l3 - mx)
    inv = 1.0 / (e1 + e2 + e3)
    w1, w2, w3 = e1 * inv, e2 * inv, e3 * inv
    parts = [a_ref[0].astype(F32)]
    for h in range(DIL_HEADS):
        r = slice(64 * h, 64 * h + 64)
        parts.append(w1[h:h + 1] * o1_ref[0, r, :] + w2[h:h + 1] * o2_ref[0, r, :] + w3[h:h + 1] * o3_ref[0, r, :])
    parts.append(c_ref[0].astype(F32))
    cat_t = jnp.concatenate(parts, axis=0)
    cat = cat_t.T.astype(BF16)
    mix = jnp.dot(cat, w_ref[...], preferred_element_type=F32)
    y = alpha * x_ref[0] + mix
    out_ref[0] = _layer_norm_rows(y, g_ref[...], b_ref[...])


def _outproj(a_t, c_t, o_list, l_list, x, w_out, g, b, alpha):
    B, S, D = x.shape
    tm = OUT_TM
    tok = lambda rows: pl.BlockSpec((1, rows, tm), lambda bb, i: (bb, 0, i))
    full = lambda shape: pl.BlockSpec(shape, lambda bb, i: (0,) * len(shape))
    nb = DIL_HEADS * HEAD_DIM
    return pl.pallas_call(
        functools.partial(_outproj_kernel, alpha=alpha),
        grid=(B, S // tm),
        in_specs=[tok(a_t.shape[1]), tok(c_t.shape[1]), tok(nb), tok(nb), tok(nb),
                  tok(DIL_HEADS), tok(DIL_HEADS), tok(DIL_HEADS),
                  pl.BlockSpec((1, tm, D), lambda bb, i: (bb, i, 0)),
                  full(w_out.shape), full(g.shape), full(b.shape)],
        out_specs=pl.BlockSpec((1, tm, D), lambda bb, i: (bb, i, 0)),
        out_shape=jax.ShapeDtypeStruct((B, S, D), F32),
        compiler_params=_cparams(2),
        name="outproj_ln",
    )(a_t, c_t, *o_list, *l_list, x, w_out, g, b)


def _ffn_kernel(x_ref, wg_ref, wu_ref, wd_ref, g_ref, b_ref, out_ref, xb_ref, acc_ref, *, alpha):
    f = pl.program_id(1)

    @pl.when(f == 0)
    def _():
        xb_ref[...] = x_ref[...].astype(BF16)
        acc_ref[...] = jnp.zeros_like(acc_ref)

    xb = xb_ref[...]
    gate = jnp.dot(xb, wg_ref[...], preferred_element_type=F32)
    up = jnp.dot(xb, wu_ref[...], preferred_element_type=F32)
    hmid = (gate * jax.nn.sigmoid(gate) * up).astype(BF16)
    acc_ref[...] += jnp.dot(hmid, wd_ref[...], preferred_element_type=F32)

    @pl.when(f == pl.num_programs(1) - 1)
    def _():
        y = alpha * x_ref[...] + acc_ref[...]
        out_ref[...] = _layer_norm_rows(y, g_ref[...], b_ref[...])


def _ffn(x2, wg, wu, wd, g, b, alpha):
    N, D = x2.shape
    tm, tf = FFN_TM, FFN_TF
    return pl.pallas_call(
        functools.partial(_ffn_kernel, alpha=alpha),
        grid=(N // tm, D_FF // tf),
        in_specs=[pl.BlockSpec((tm, D), lambda i, f: (i, 0)),
                  pl.BlockSpec((D, tf), lambda i, f: (0, f)),
                  pl.BlockSpec((D, tf), lambda i, f: (0, f)),
                  pl.BlockSpec((tf, D), lambda i, f: (f, 0)),
                  pl.BlockSpec((1, D), lambda i, f: (0, 0)),
                  pl.BlockSpec((1, D), lambda i, f: (0, 0))],
        out_specs=pl.BlockSpec((tm, D), lambda i, f: (i, 0)),
        out_shape=jax.ShapeDtypeStruct((N, D), F32),
        scratch_shapes=[pltpu.VMEM((tm, D), BF16), pltpu.VMEM((tm, D), F32)],
        compiler_params=_cparams(2),
        name="ffn_ln",
    )(x2, wg, wu, wd, g, b)


def _rope_tables(S):
    inv = ROPE_THETA ** (-jnp.arange(0, 32, 2, dtype=F32) / 32)
    pos = jnp.arange(S, dtype=F32)
    row = jnp.repeat(jnp.arange(S // GRID_W), GRID_W).astype(F32)
    col = jnp.tile(jnp.arange(GRID_W), S // GRID_W).astype(F32)
    tabs = []
    for p in (pos, row, col):
        ang = inv[:, None] * p[None, :]
        tabs += [jnp.cos(ang), jnp.sin(ang)]
    return jnp.stack(tabs, axis=0)


def kernel(x, w_in, mla_q_norm, mla_kv_norm, mla_w_uq, mla_w_ukv, gqa_q_norm, gqa_k_norm, rel_bias, w_out,
           ln1_g, ln1_b, ffn_w_gate, ffn_w_up, ffn_w_down, ln2_g, ln2_b):
    B, S, D = x.shape
    depth = w_in.shape[0]
    alpha = (2.0 * depth) ** 0.25
    tabs = _rope_tables(S)
    bias_tabs = [_dil_bias_table(rel_bias, window, dil) for window, dil in DIL_BRANCHES]

    for l in range(depth):
        (qa, ka, va, qb, kb, vb, qc, kc, vc) = _prep(
            x, w_in[l].T.astype(BF16), mla_w_uq[l].T.astype(BF16), mla_w_ukv[l].T.astype(BF16),
            mla_q_norm[l][:, None], mla_kv_norm[l][:, None], gqa_q_norm[l][:, None], gqa_k_norm[l][:, None], tabs)
        out_a = _flash(qa, ka, va, MLA_HEADS, MLA_HEADS, shared_k=False)
        out_c = _flash(qc, kc, vc, GQA_Q_HEADS, GQA_KV_HEADS, shared_k=True)
        o_list, l_list = [], []
        for (window, dil), bias_t in zip(DIL_BRANCHES, bias_tabs):
            o_b, lse_b = _dil_branch(_to_sub(qb, dil), _to_sub(kb, dil), _to_sub(vb, dil), bias_t)
            o_list.append(_from_sub(o_b, dil))
            l_list.append(_from_sub(lse_b.reshape(lse_b.shape[0], DIL_HEADS, -1), dil))
        x = _outproj(out_a, out_c, o_list, l_list, x, w_out[l].astype(BF16),
                     ln1_g[l][None, :], ln1_b[l][None, :], alpha)
        x = _ffn(x.reshape(B * S, D), ffn_w_gate[l].astype(BF16), ffn_w_up[l].astype(BF16),
                 ffn_w_down[l].astype(BF16), ln2_g[l][None, :], ln2_b[l][None, :], alpha).reshape(B, S, D)
    return x
```

```python
import functools
import math

import jax
import jax.numpy as jnp
from jax import lax
from jax.experimental import pallas as pl
from jax.experimental.pallas import tpu as pltpu

F32 = jnp.float32
BF16 = jnp.bfloat16

D_MODEL = 1024
HEAD_DIM = 64
GRID_W = 64
ROPE_THETA = 10000.0
MLA_HEADS = 6
MLA_Q_RANK = 256
MLA_KV_RANK = 128
MLA_NOPE_DIM = 64
MLA_ROPE_DIM = 32
MLA_V_DIM = 64
DIL_HEADS = 6
DIL_BRANCHES = ((128, 1), (512, 4), (2048, 16))
GQA_Q_HEADS = 4
GQA_KV_HEADS = 2
REL_BUCKETS = 32
REL_MAX_DIST = 1024
D_FF = 2816
MLA_IN = MLA_Q_RANK + MLA_KV_RANK + MLA_ROPE_DIM
DIL_IN = 3 * DIL_HEADS * HEAD_DIM
GQA_IN = (GQA_Q_HEADS + 2 * GQA_KV_HEADS) * HEAD_DIM
NEG_INF = -1e30
LOG2E = math.log2(math.e)

V7X_LANES = 128
V7X_VMEM_LIMIT_BYTES = 56 * 1024 * 1024

KPAD = V7X_LANES

PREP_TM = 512
ATT_TQ = 1024
ATT_CHAIN = 256
ATT_TK = 256
ATT_UNROLL = 1
ATT_FLAGS = None
DIL_TQ = 256
DIL_HALO = 128
OUT_TM = 512
FFN_TM = 1024
FFN_TF = 256


def _cparams(n_axes, flags=None):
    return pltpu.CompilerParams(
        dimension_semantics=("arbitrary",) * n_axes,
        vmem_limit_bytes=V7X_VMEM_LIMIT_BYTES,
        flags=flags,
    )


def _rms_t(h, g, eps=1e-6):
    ms = jnp.mean(h * h, axis=0, keepdims=True)
    return h * lax.rsqrt(ms + eps) * g


def _prep_kernel(x_ref, w_ref, wuq_ref, wukv_ref, gq_ref, gkv_ref, gcq_ref, gck_ref, tab_ref,
                 qa_ref, ka_ref, va_ref, qb_ref, kb_ref, vb_ref, qc_ref, kc_ref, vc_ref, *, dtype_k):
    xb = x_ref[0].astype(BF16)
    tm = xb.shape[0]

    def proj(r0, r1):
        return lax.dot_general(w_ref[r0:r1, :], xb, (((1,), (1,)), ((), ())),
                               preferred_element_type=F32)

    cos_p, sin_p = tab_ref[0], tab_ref[1]
    cos_r, sin_r = tab_ref[2], tab_ref[3]
    cos_c, sin_c = tab_ref[4], tab_ref[5]
    zpad_bf = jnp.zeros((KPAD - 96, tm), BF16)
    zpad_k = jnp.zeros((KPAD - 96, tm), dtype_k)

    sc_a = (MLA_NOPE_DIM + MLA_ROPE_DIM) ** -0.5 * LOG2E
    hq = proj(0, MLA_Q_RANK)
    cq = _rms_t(hq, gq_ref[...]).astype(BF16)
    qa = jnp.dot(wuq_ref[...], cq, preferred_element_type=F32)
    hkv = proj(MLA_Q_RANK, MLA_Q_RANK + MLA_KV_RANK)
    ckv = _rms_t(hkv, gkv_ref[...]).astype(BF16)
    kva = jnp.dot(wukv_ref[...], ckv, preferred_element_type=F32)
    hkr = proj(MLA_Q_RANK + MLA_KV_RANK, MLA_IN)
    kr1, kr2 = hkr[0:16], hkr[16:32]
    krope1 = (kr1 * cos_p - kr2 * sin_p).astype(dtype_k)
    krope2 = (kr1 * sin_p + kr2 * cos_p).astype(dtype_k)
    for h in range(MLA_HEADS):
        qb0 = 96 * h
        ob = KPAD * h
        r1, r2 = qa[qb0 + 64:qb0 + 80], qa[qb0 + 80:qb0 + 96]
        qa_ref[0, ob:ob + 64, :] = (qa[qb0:qb0 + 64] * sc_a).astype(BF16)
        qa_ref[0, ob + 64:ob + 80, :] = ((r1 * cos_p - r2 * sin_p) * sc_a).astype(BF16)
        qa_ref[0, ob + 80:ob + 96, :] = ((r1 * sin_p + r2 * cos_p) * sc_a).astype(BF16)
        qa_ref[0, ob + 96:ob + KPAD, :] = zpad_bf
        kb0 = 128 * h
        ka_ref[0, ob:ob + 64, :] = kva[kb0:kb0 + 64].astype(dtype_k)
        ka_ref[0, ob + 64:ob + 80, :] = krope1
        ka_ref[0, ob + 80:ob + 96, :] = krope2
        ka_ref[0, ob + 96:ob + KPAD, :] = zpad_k
        va_ref[0, 64 * h:64 * h + 64, :] = kva[kb0 + 64:kb0 + 128].astype(BF16)

    sc_b = HEAD_DIM ** -0.5 * LOG2E
    o0 = MLA_IN
    nb = DIL_HEADS * HEAD_DIM
    hbq = proj(o0, o0 + nb)
    hbk = proj(o0 + nb, o0 + 2 * nb)
    hbv = proj(o0 + 2 * nb, o0 + 3 * nb)
    z64_bf = jnp.zeros((KPAD - 64, tm), BF16)
    z64_k = jnp.zeros((KPAD - 64, tm), dtype_k)
    for h in range(DIL_HEADS):
        ob = KPAD * h
        qb_ref[0, ob:ob + 64, :] = (hbq[64 * h:64 * h + 64] * sc_b).astype(BF16)
        qb_ref[0, ob + 64:ob + KPAD, :] = z64_bf
        kb_ref[0, ob:ob + 64, :] = hbk[64 * h:64 * h + 64].astype(dtype_k)
        kb_ref[0, ob + 64:ob + KPAD, :] = z64_k
    vb_ref[0] = hbv.astype(BF16)

    sc_c = HEAD_DIM ** -0.5 * LOG2E
    o0 = MLA_IN + DIL_IN
    nqk = (GQA_Q_HEADS + GQA_KV_HEADS) * HEAD_DIM
    hqk = proj(o0, o0 + nqk)
    hv = proj(o0 + nqk, o0 + GQA_IN)

    def norm_rope(xh, g):
        y = _rms_t(xh, g)
        a1, a2, b1, b2 = y[0:16], y[16:32], y[32:48], y[48:64]
        return (a1 * cos_r - a2 * sin_r, a1 * sin_r + a2 * cos_r,
                b1 * cos_c - b2 * sin_c, b1 * sin_c + b2 * cos_c)

    group = GQA_Q_HEADS // GQA_KV_HEADS
    for j in range(GQA_Q_HEADS):
        parts = norm_rope(hqk[64 * j:64 * j + 64], gcq_ref[...])
        g = j // group
        for gg in range(GQA_KV_HEADS):
            ob = KPAD * j + 64 * gg
            if gg == g:
                for t, part in enumerate(parts):
                    qc_ref[0, ob + 16 * t:ob + 16 * t + 16, :] = (part * sc_c).astype(BF16)
            else:
                qc_ref[0, ob:ob + 64, :] = z64_bf
    for j in range(GQA_KV_HEADS):
        r0 = 64 * (GQA_Q_HEADS + j)
        parts = norm_rope(hqk[r0:r0 + 64], gck_ref[...])
        for t, part in enumerate(parts):
            kc_ref[0, 64 * j + 16 * t:64 * j + 16 * t + 16, :] = part.astype(dtype_k)
    vc_ref[0] = hv.astype(BF16)


def _prep(x, w_in_t, wuq_t, wukv_t, gq, gkv, gcq, gck, tabs):
    B, S, D = x.shape
    tm = PREP_TM
    dtype_k = F32
    full = lambda shape: pl.BlockSpec(shape, lambda b, i: (0,) * len(shape))
    tok = lambda rows: pl.BlockSpec((1, rows, tm), lambda b, i: (b, 0, i))
    out_rows = [(MLA_HEADS * KPAD, BF16), (MLA_HEADS * KPAD, dtype_k), (MLA_HEADS * MLA_V_DIM, BF16),
                (DIL_HEADS * KPAD, BF16), (DIL_HEADS * KPAD, dtype_k), (DIL_HEADS * HEAD_DIM, BF16),
                (GQA_Q_HEADS * KPAD, BF16), (GQA_KV_HEADS * HEAD_DIM, dtype_k), (GQA_KV_HEADS * HEAD_DIM, BF16)]
    return pl.pallas_call(
        functools.partial(_prep_kernel, dtype_k=dtype_k),
        grid=(B, S // tm),
        in_specs=[pl.BlockSpec((1, tm, D), lambda b, i: (b, i, 0)),
                  full(w_in_t.shape), full(wuq_t.shape), full(wukv_t.shape),
                  full(gq.shape), full(gkv.shape), full(gcq.shape), full(gck.shape),
                  pl.BlockSpec((6, 16, tm), lambda b, i: (0, 0, i))],
        out_specs=[tok(r) for r, _ in out_rows],
        out_shape=[jax.ShapeDtypeStruct((B, r, S), dt) for r, dt in out_rows],
        compiler_params=_cparams(2),
        name="prep",
    )(x, w_in_t, wuq_t, wukv_t, gq, gkv, gcq, gck, tabs)


def _flash_kernel(q_ref, k_ref, v_ref, o_ref, knat_ref, s_ref, acc_ref, *, seq, tk, shared_k):
    h = pl.program_id(1)
    i = pl.program_id(2)
    first = (i == 0) & (h == 0) if shared_k else (i == 0)

    @pl.when(first)
    def _():
        chunk = 512
        for c in range(seq // chunk):
            knat_ref[c * chunk:(c + 1) * chunk, :] = k_ref[0, :, c * chunk:(c + 1) * chunk].T.astype(BF16)

    tq = q_ref.shape[2]
    dv = v_ref.shape[1]
    cw = ATT_CHAIN
    n_chain = tq // cw

    nk = seq // tk

    def logits(j, c):
        off = pl.multiple_of(j * tk, tk)
        s_ref[c] = jnp.dot(knat_ref[pl.ds(off, tk), :], q_ref[0, :, c * cw:(c + 1) * cw],
                           preferred_element_type=F32)

    def softmax_pv(j, c, m, l):
        off = pl.multiple_of(j * tk, tk)
        s = s_ref[c]
        m_new = jnp.maximum(m, jnp.max(s, axis=0, keepdims=True))
        alpha = jnp.exp2(m - m_new)
        p = jnp.exp2(s - m_new)
        l = alpha * l + jnp.sum(p, axis=0, keepdims=True)
        acc_ref[c] = alpha * acc_ref[c] + jnp.dot(v_ref[0, :, pl.ds(off, tk)], p.astype(BF16),
                                                  preferred_element_type=F32)
        return m_new, l

    for c in range(n_chain):
        logits(0, c)
        acc_ref[c] = jnp.zeros((dv, cw), F32)

    def body(j, carry):
        out = []
        for c in range(n_chain):
            m, l = softmax_pv(j, c, *carry[c])
            logits(j + 1, c)
            out.append((m, l))
        return tuple(out)

    init = tuple((jnp.full((1, cw), NEG_INF, F32), jnp.zeros((1, cw), F32)) for _ in range(n_chain))
    carry = lax.fori_loop(0, nk - 1, body, init, unroll=ATT_UNROLL)
    for c in range(n_chain):
        m, l = softmax_pv(nk - 1, c, *carry[c])
        o_ref[0, :, c * cw:(c + 1) * cw] = (acc_ref[c] * (1.0 / l)).astype(o_ref.dtype)


def _flash(q_t, k_t, v_t, n_heads, n_kv, shared_k):
    B, _, S = q_t.shape
    dv = v_t.shape[1] // n_kv
    group = n_heads // n_kv
    tq = ATT_TQ
    k_idx = (lambda b, h, i: (b, 0, 0)) if shared_k else (lambda b, h, i: (b, h // group, 0))
    return pl.pallas_call(
        functools.partial(_flash_kernel, seq=S, tk=ATT_TK, shared_k=shared_k),
        grid=(B, n_heads, S // tq),
        in_specs=[pl.BlockSpec((1, KPAD, tq), lambda b, h, i: (b, h, i)),
                  pl.BlockSpec((1, KPAD, S), k_idx),
                  pl.BlockSpec((1, dv, S), lambda b, h, i: (b, h // group, 0))],
        out_specs=pl.BlockSpec((1, dv, tq), lambda b, h, i: (b, h, i)),
        out_shape=jax.ShapeDtypeStruct((B, n_heads * dv, S), BF16),
        scratch_shapes=[pltpu.VMEM((S, KPAD), BF16),
                        pltpu.VMEM((tq // ATT_CHAIN, ATT_TK, ATT_CHAIN), F32),
                        pltpu.VMEM((tq // ATT_CHAIN, dv, ATT_CHAIN), F32)],
        compiler_params=_cparams(3, ATT_FLAGS),
        name="flash_shared" if shared_k else "flash",
    )(q_t, k_t, v_t)


def _dil_kernel(q_ref, k_ref, v_ref, bias_ref, o_ref, lse_ref, knat_ref, vpad_ref, *, sub_len):
    i = pl.program_id(2)
    tq = q_ref.shape[2]
    win = tq + 2 * DIL_HALO
    dv = v_ref.shape[1]

    @pl.when(i == 0)
    def _():
        knat_ref[0:DIL_HALO, :] = jnp.zeros((DIL_HALO, KPAD), BF16)
        knat_ref[DIL_HALO + sub_len:2 * DIL_HALO + sub_len, :] = jnp.zeros((DIL_HALO, KPAD), BF16)
        chunk = 256
        for c in range(sub_len // chunk):
            knat_ref[DIL_HALO + c * chunk:DIL_HALO + (c + 1) * chunk, :] = (
                k_ref[0, :, c * chunk:(c + 1) * chunk].T.astype(BF16))
        vpad_ref[:, 0:DIL_HALO] = jnp.zeros((dv, DIL_HALO), BF16)
        vpad_ref[:, DIL_HALO + sub_len:2 * DIL_HALO + sub_len] = jnp.zeros((dv, DIL_HALO), BF16)
        vpad_ref[:, DIL_HALO:DIL_HALO + sub_len] = v_ref[0]

    l0 = pl.multiple_of(i * tq, tq)
    s = jnp.dot(knat_ref[pl.ds(l0, win), :], q_ref[0], preferred_element_type=F32)
    key = l0 - DIL_HALO + lax.broadcasted_iota(jnp.int32, (win, tq), 0)
    valid = (key >= 0) & (key < sub_len)
    s = jnp.where(valid, s + bias_ref[0], NEG_INF)
    m = jnp.max(s, axis=0, keepdims=True)
    p = jnp.exp2(s - m)
    l = jnp.sum(p, axis=0, keepdims=True)
    o = jnp.dot(vpad_ref[:, pl.ds(l0, win)], p.astype(BF16), preferred_element_type=F32)
    o_ref[0] = o * (1.0 / l)
    lse_ref[0, 0] = m + jnp.log2(l)


def _dil_branch(q_t, k_t, v_t, bias_t):
    BB, _, L = q_t.shape
    tq = DIL_TQ
    win = tq + 2 * DIL_HALO
    H = DIL_HEADS
    return pl.pallas_call(
        functools.partial(_dil_kernel, sub_len=L),
        grid=(BB, H, L // tq),
        in_specs=[pl.BlockSpec((1, KPAD, tq), lambda b, h, i: (b, h, i)),
                  pl.BlockSpec((1, KPAD, L), lambda b, h, i: (b, h, 0)),
                  pl.BlockSpec((1, HEAD_DIM, L), lambda b, h, i: (b, h, 0)),
                  pl.BlockSpec((1, win, tq), lambda b, h, i: (h, 0, 0))],
        out_specs=[pl.BlockSpec((1, HEAD_DIM, tq), lambda b, h, i: (b, h, i)),
                   pl.BlockSpec((1, 1, 1, tq), lambda b, h, i: (b, h, 0, i))],
        out_shape=[jax.ShapeDtypeStruct((BB, H * HEAD_DIM, L), F32),
                   jax.ShapeDtypeStruct((BB, H, 1, L), F32)],
        scratch_shapes=[pltpu.VMEM((L + 2 * DIL_HALO, KPAD), BF16),
                        pltpu.VMEM((HEAD_DIM, L + 2 * DIL_HALO), BF16)],
        compiler_params=_cparams(3),
        name="dilated",
    )(q_t, k_t, v_t, bias_t)


def _t5_bucket(rel):
    nb = REL_BUCKETS // 2
    exact = nb // 2
    ret = jnp.where(rel > 0, nb, 0)
    n = jnp.abs(rel)
    nf = jnp.maximum(n, 1).astype(F32)
    large = exact + (jnp.log(nf / exact) / math.log(REL_MAX_DIST / exact) * (nb - exact)).astype(jnp.int32)
    large = jnp.minimum(large, nb - 1)
    return ret + jnp.where(n < exact, n, large)


def _dil_bias_table(rel_bias, window, dil):
    half = window // (2 * dil)
    tq = DIL_TQ
    win = tq + 2 * DIL_HALO
    rel = jnp.arange(win)[:, None] - DIL_HALO - jnp.arange(tq)[None, :]
    bucket = _t5_bucket(rel * dil)
    rb = rel_bias.astype(F32) * LOG2E
    bias = jnp.zeros((DIL_HEADS, win, tq), F32)
    for b in range(REL_BUCKETS):
        bias = jnp.where((bucket == b)[None], rb[b][:, None, None], bias)
    return jnp.where((jnp.abs(rel) <= half)[None], bias, NEG_INF)


def _to_sub(t, dil):
    if dil == 1:
        return t
    B, R, S = t.shape
    return t.reshape(B, R, S // dil, dil).transpose(0, 3, 1, 2).reshape(B * dil, R, S // dil)


def _from_sub(t, dil):
    if dil == 1:
        return t
    BB, R, L = t.shape
    return t.reshape(BB // dil, dil, R, L).transpose(0, 2, 3, 1).reshape(BB // dil, R, L * dil)


def _layer_norm_rows(y, g, b, eps=1e-5):
    mu = jnp.mean(y, axis=-1, keepdims=True)
    yc = y - mu
    var = jnp.mean(yc * yc, axis=-1, keepdims=True)
    return yc * lax.rsqrt(var + eps) * g + b


def _outproj_kernel(a_ref, c_ref, o1_ref, o2_ref, o3_ref, l1_ref, l2_ref, l3_ref, x_ref, w_ref, g_ref, b_ref,
                    out_ref, *, alpha):
    l1, l2, l3 = l1_ref[0], l2_ref[0], l3_ref[0]
    mx = jnp.maximum(jnp.maximum(l1, l2), l3)
    e1, e2, e3 = jnp.exp2(l1 - mx), jnp.exp2(l2 - mx), jnp.exp2(l3 - mx)
    inv = 1.0 / (e1 + e2 + e3)
    w1, w2, w3 = e1 * inv, e2 * inv, e3 * inv
    parts = [a_ref[0].astype(F32)]
    for h in range(DIL_HEADS):
        r = slice(64 * h, 64 * h + 64)
        parts.append(w1[h:h + 1] * o1_ref[0, r, :] + w2[h:h + 1] * o2_ref[0, r, :] + w3[h:h + 1] * o3_ref[0, r, :])
    parts.append(c_ref[0].astype(F32))
    cat_t = jnp.concatenate(parts, axis=0)
    cat = cat_t.T.astype(BF16)
    mix = jnp.dot(cat, w_ref[...], preferred_element_type=F32)
    y = alpha * x_ref[0] + mix
    out_ref[0] = _layer_norm_rows(y, g_ref[...], b_ref[...])


def _outproj(a_t, c_t, o_list, l_list, x, w_out, g, b, alpha):
    B, S, D = x.shape
    tm = OUT_TM
    tok = lambda rows: pl.BlockSpec((1, rows, tm), lambda bb, i: (bb, 0, i))
    full = lambda shape: pl.BlockSpec(shape, lambda bb, i: (0,) * len(shape))
    nb = DIL_HEADS * HEAD_DIM
    return pl.pallas_call(
        functools.partial(_outproj_kernel, alpha=alpha),
        grid=(B, S // tm),
        in_specs=[tok(a_t.shape[1]), tok(c_t.shape[1]), tok(nb), tok(nb), tok(nb),
                  tok(DIL_HEADS), tok(DIL_HEADS), tok(DIL_HEADS),
                  pl.BlockSpec((1, tm, D), lambda bb, i: (bb, i, 0)),
                  full(w_out.shape), full(g.shape), full(b.shape)],
        out_specs=pl.BlockSpec((1, tm, D), lambda bb, i: (bb, i, 0)),
        out_shape=jax.ShapeDtypeStruct((B, S, D), F32),
        compiler_params=_cparams(2),
        name="outproj_ln",
    )(a_t, c_t, *o_list, *l_list, x, w_out, g, b)


def _ffn_kernel(x_ref, wg_ref, wu_ref, wd_ref, g_ref, b_ref, out_ref, xb_ref, acc_ref, *, alpha):
    f = pl.program_id(1)

    @pl.when(f == 0)
    def _():
        xb_ref[...] = x_ref[...].astype(BF16)
        acc_ref[...] = jnp.zeros_like(acc_ref)

    xb = xb_ref[...]
    gate = jnp.dot(xb, wg_ref[...], preferred_element_type=F32)
    up = jnp.dot(xb, wu_ref[...], preferred_element_type=F32)
    hmid = (gate * jax.nn.sigmoid(gate) * up).astype(BF16)
    acc_ref[...] += jnp.dot(hmid, wd_ref[...], preferred_element_type=F32)

    @pl.when(f == pl.num_programs(1) - 1)
    def _():
        y = alpha * x_ref[...] + acc_ref[...]
        out_ref[...] = _layer_norm_rows(y, g_ref[...], b_ref[...])


def _ffn(x2, wg, wu, wd, g, b, alpha):
    N, D = x2.shape
    tm, tf = FFN_TM, FFN_TF
    return pl.pallas_call(
        functools.partial(_ffn_kernel, alpha=alpha),
        grid=(N // tm, D_FF // tf),
        in_specs=[pl.BlockSpec((tm, D), lambda i, f: (i, 0)),
                  pl.BlockSpec((D, tf), lambda i, f: (0, f)),
                  pl.BlockSpec((D, tf), lambda i, f: (0, f)),
                  pl.BlockSpec((tf, D), lambda i, f: (f, 0)),
                  pl.BlockSpec((1, D), lambda i, f: (0, 0)),
                  pl.BlockSpec((1, D), lambda i, f: (0, 0))],
        out_specs=pl.BlockSpec((tm, D), lambda i, f: (i, 0)),
        out_shape=jax.ShapeDtypeStruct((N, D), F32),
        scratch_shapes=[pltpu.VMEM((tm, D), BF16), pltpu.VMEM((tm, D), F32)],
        compiler_params=_cparams(2),
        name="ffn_ln",
    )(x2, wg, wu, wd, g, b)


def _rope_tables(S):
    inv = ROPE_THETA ** (-jnp.arange(0, 32, 2, dtype=F32) / 32)
    pos = jnp.arange(S, dtype=F32)
    row = jnp.repeat(jnp.arange(S // GRID_W), GRID_W).astype(F32)
    col = jnp.tile(jnp.arange(GRID_W), S // GRID_W).astype(F32)
    tabs = []
    for p in (pos, row, col):
        ang = inv[:, None] * p[None, :]
        tabs += [jnp.cos(ang), jnp.sin(ang)]
    return jnp.stack(tabs, axis=0)


def kernel(x, w_in, mla_q_norm, mla_kv_norm, mla_w_uq, mla_w_ukv, gqa_q_norm, gqa_k_norm, rel_bias, w_out,
           ln1_g, ln1_b, ffn_w_gate, ffn_w_up, ffn_w_down, ln2_g, ln2_b):
    B, S, D = x.shape
    depth = w_in.shape[0]
    alpha = (2.0 * depth) ** 0.25
    tabs = _rope_tables(S)
    bias_tabs = [_dil_bias_table(rel_bias, window, dil) for window, dil in DIL_BRANCHES]

    for l in range(depth):
        (qa, ka, va, qb, kb, vb, qc, kc, vc) = _prep(
            x, w_in[l].T.astype(BF16), mla_w_uq[l].T.astype(BF16), mla_w_ukv[l].T.astype(BF16),
            mla_q_norm[l][:, None], mla_kv_norm[l][:, None], gqa_q_norm[l][:, None], gqa_k_norm[l][:, None], tabs)
        out_a = _flash(qa, ka, va, MLA_HEADS, MLA_HEADS, shared_k=False)
        out_c = _flash(qc, kc, vc, GQA_Q_HEADS, GQA_KV_HEADS, shared_k=True)
        o_list, l_list = [], []
        for (window, dil), bias_t in zip(DIL_BRANCHES, bias_tabs):
            o_b, lse_b = _dil_branch(_to_sub(qb, dil), _to_sub(kb, dil), _to_sub(vb, dil), bias_t)
            o_list.append(_from_sub(o_b, dil))
            l_list.append(_from_sub(lse_b.reshape(lse_b.shape[0], DIL_HEADS, -1), dil))
        x = _outproj(out_a, out_c, o_list, l_list, x, w_out[l].astype(BF16),
                     ln1_g[l][None, :], ln1_b[l][None, :], alpha)
        x = _ffn(x.reshape(B * S, D), ffn_w_gate[l].astype(BF16), ffn_w_up[l].astype(BF16),
                 ffn_w_down[l].astype(BF16), ln2_g[l][None, :], ln2_b[l][None, :], alpha).reshape(B, S, D)
    return x
```
